```python
import jax, jax.numpy as jnp
from jax import lax
import numpy as np

D_MODEL = 1024
BATCH = 8
SEQ = 4096
DEPTH = 2
DEC_BATCH = 8
DEC_SEQ = 64
PAST_LEN = 4096

CHUNK = 64
D_MIX = D_MODEL
W_A = D_MIX // 2
W_B = D_MIX - W_A
N_HEADS_A = 8
HEAD_DIM_A = W_A // N_HEADS_A
GMLP_CHUNK = 128
CONV_WIDTH = 31
CONV_STATE = CONV_WIDTH - 1
D_FF = 2816
N_EXPERTS = 8
TOP_K = 2
D_FF_EXPERT = 3584
N_DENSE = (DEPTH + 1) // 2
N_MOE = DEPTH // 2
RMS_EPS = 1e-6
LN_EPS = 1e-5

kernel_name = 'hybrid_gmlp_conformer_stream_step'


def rmsnorm(x, g):
    x32 = x.astype(jnp.float32)
    y = x32 * lax.rsqrt(jnp.mean(x32 * x32, axis=-1, keepdims=True) + RMS_EPS)
    return (y * g.astype(jnp.float32)).astype(x.dtype)


def layernorm(x, g, b):
    x32 = x.astype(jnp.float32)
    mu = jnp.mean(x32, axis=-1, keepdims=True)
    xc = x32 - mu
    y = xc * lax.rsqrt(jnp.mean(xc * xc, axis=-1, keepdims=True) + LN_EPS)
    return (y * g.astype(jnp.float32) + b.astype(jnp.float32)).astype(x.dtype)


def gmlp_spatial(z, ln_g, ln_b, ws, bs):
    u, v = jnp.split(z, 2, axis=-1)
    v = layernorm(v, ln_g, ln_b)
    bsz, s_len, _ = v.shape
    L = min(s_len, GMLP_CHUNK)
    vc = v.reshape(bsz, s_len // L, L, N_HEADS_A, HEAD_DIM_A)
    mask = jnp.tril(jnp.ones((L, L), dtype=bool))
    w = jnp.where(mask[None], ws[:, :L, :L], 0.0).astype(v.dtype)
    bias = jnp.transpose(bs[:, :L])[None, None, :, :, None].astype(v.dtype)
    s = jnp.einsum('hij,bnjhd->bnihd', w, vc) + bias
    return u * s.reshape(bsz, s_len, W_A), v


def conv_module(zb, prev, conv_w, conv_b, ln_g, ln_b):
    a, gate = jnp.split(zb, 2, axis=-1)
    h = a * jax.nn.sigmoid(gate)
    hp = jnp.concatenate([prev.astype(h.dtype), h], axis=1)
    y = lax.conv_general_dilated(hp, conv_w[:, None, :].astype(h.dtype), window_strides=(1,),
                                 padding='VALID', dimension_numbers=('NWC', 'WIO', 'NWC'),
                                 feature_group_count=W_B)
    y = y + conv_b.astype(h.dtype)
    y = jax.nn.silu(layernorm(y, ln_g, ln_b))
    return y, hp[:, -CONV_STATE:, :]


def dense_swiglu(h, wg, wu, wd):
    return (jax.nn.silu(h @ wg) * (h @ wu)) @ wd


def moe_swiglu(h, router, wg, wu, wd):
    bsz, s_len, d = h.shape
    t = h.reshape(bsz * s_len, d)
    logits = (t @ router).astype(jnp.float32)
    top_v, top_i = lax.top_k(logits, TOP_K)
    top_w = jax.nn.softmax(top_v, axis=-1)
    gate = jnp.sum(jax.nn.one_hot(top_i, N_EXPERTS, dtype=jnp.float32) * top_w[..., None], axis=1)
    out = jnp.zeros_like(t)
    for e in range(N_EXPERTS):
        he = jax.nn.silu(t @ wg[e]) * (t @ wu[e])
        out = out + gate[:, e:e + 1].astype(t.dtype) * (he @ wd[e])
    return out.reshape(bsz, s_len, d)


def trunk(x, conv_prev, norm_mix_g, w_in, gmlp_ln_g, gmlp_ln_b, gmlp_ws, gmlp_bs,
          conv_w, conv_b, conv_ln_g, conv_ln_b, w_out, norm_ffn_g,
          ffn_w_gate, ffn_w_up, ffn_w_down, moe_router, moe_w_gate, moe_w_up, moe_w_down,
          final_norm_g):
    conv_states = []
    v_rows = []
    for l in range(DEPTH):
        h = rmsnorm(x, norm_mix_g[l])
        z = h @ w_in[l]
        a_out, v = gmlp_spatial(jax.nn.gelu(z[..., :2 * W_A]), gmlp_ln_g[l], gmlp_ln_b[l],
                                gmlp_ws[l], gmlp_bs[l])
        b_out, cst = conv_module(z[..., 2 * W_A:], conv_prev[l], conv_w[l], conv_b[l],
                                 conv_ln_g[l], conv_ln_b[l])
        x = x + jnp.concatenate([a_out, b_out], axis=-1) @ w_out[l]
        h = rmsnorm(x, norm_ffn_g[l])
        if l % 2 == 0:
            i = l // 2
            x = x + dense_swiglu(h, ffn_w_gate[i], ffn_w_up[i], ffn_w_down[i])
        else:
            i = l // 2
            x = x + moe_swiglu(h, moe_router[i], moe_w_gate[i], moe_w_up[i], moe_w_down[i])
        conv_states.append(cst)
        v_rows.append(v)
    return rmsnorm(x, final_norm_g), jnp.stack(conv_states), jnp.stack(v_rows)


def setup_inputs(seed: int = 0) -> dict:
    key = jax.random.key(seed)
    ks = jax.random.split(key, 24)
    f32 = jnp.float32
    nrm = lambda k, shape, scale: jax.random.normal(k, shape, f32) * scale
    return {
        'x_prompt': nrm(ks[0], (BATCH, SEQ, D_MODEL), 1.0),
        'x_sample': nrm(ks[1], (DEC_BATCH, DEC_SEQ, D_MODEL), 1.0),
        'cache_conv': nrm(ks[2], (DEPTH, DEC_BATCH, CONV_STATE, W_B), 0.5),
        'norm_mix_g': 1.0 + nrm(ks[3], (DEPTH, D_MODEL), 0.05),
        'w_in': nrm(ks[4], (DEPTH, D_MODEL, 2 * W_A + 2 * W_B), D_MODEL ** -0.5),
        'gmlp_ln_g': 1.0 + nrm(ks[5], (DEPTH, W_A), 0.05),
        'gmlp_ln_b': nrm(ks[6], (DEPTH, W_A), 0.02),
        'gmlp_ws': nrm(ks[7], (DEPTH, N_HEADS_A, GMLP_CHUNK, GMLP_CHUNK), 0.5 * GMLP_CHUNK ** -0.5),
        'gmlp_bs': 1.0 + nrm(ks[8], (DEPTH, N_HEADS_A, GMLP_CHUNK), 0.1),
        'conv_w': nrm(ks[9], (DEPTH, CONV_WIDTH, W_B), CONV_WIDTH ** -0.5),
        'conv_b': nrm(ks[10], (DEPTH, W_B), 0.02),
        'conv_ln_g': 1.0 + nrm(ks[11], (DEPTH, W_B), 0.05),
        'conv_ln_b': nrm(ks[12], (DEPTH, W_B), 0.02),
        'w_out': nrm(ks[13], (DEPTH, D_MIX, D_MODEL), D_MIX ** -0.5),
        'norm_ffn_g': 1.0 + nrm(ks[14], (DEPTH, D_MODEL), 0.05),
        'ffn_w_gate': nrm(ks[15], (N_DENSE, D_MODEL, D_FF), D_MODEL ** -0.5),
        'ffn_w_up': nrm(ks[16], (N_DENSE, D_MODEL, D_FF), D_MODEL ** -0.5),
        'ffn_w_down': nrm(ks[17], (N_DENSE, D_FF, D_MODEL), D_FF ** -0.5),
        'moe_router': nrm(ks[18], (N_MOE, D_MODEL, N_EXPERTS), D_MODEL ** -0.5),
        'moe_w_gate': nrm(ks[19], (N_MOE, N_EXPERTS, D_MODEL, D_FF_EXPERT), D_MODEL ** -0.5),
        'moe_w_up': nrm(ks[20], (N_MOE, N_EXPERTS, D_MODEL, D_FF_EXPERT), D_MODEL ** -0.5),
        'moe_w_down': nrm(ks[21], (N_MOE, N_EXPERTS, D_FF_EXPERT, D_MODEL), D_FF_EXPERT ** -0.5),
        'final_norm_g': 1.0 + nrm(ks[22], (D_MODEL,), 0.05),
    }


def reference(x_prompt, x_sample, cache_conv, norm_mix_g, w_in, gmlp_ln_g, gmlp_ln_b, gmlp_ws,
              gmlp_bs, conv_w, conv_b, conv_ln_g, conv_ln_b, w_out, norm_ffn_g,
              ffn_w_gate, ffn_w_up, ffn_w_down, moe_router, moe_w_gate, moe_w_up, moe_w_down,
              final_norm_g):
    zero_prev = jnp.zeros((DEPTH, x_prompt.shape[0], CONV_STATE, W_B), x_prompt.dtype)
    y_prompt, state_conv_prompt, _ = trunk(
        x_prompt, zero_prev, norm_mix_g, w_in, gmlp_ln_g, gmlp_ln_b, gmlp_ws, gmlp_bs,
        conv_w, conv_b, conv_ln_g, conv_ln_b, w_out, norm_ffn_g,
        ffn_w_gate, ffn_w_up, ffn_w_down, moe_router, moe_w_gate, moe_w_up, moe_w_down,
        final_norm_g)
    y_sample, state_conv_sample, state_gmlp_v_sample = trunk(
        x_sample, cache_conv, norm_mix_g, w_in, gmlp_ln_g, gmlp_ln_b, gmlp_ws, gmlp_bs,
        conv_w, conv_b, conv_ln_g, conv_ln_b, w_out, norm_ffn_g,
        ffn_w_gate, ffn_w_up, ffn_w_down, moe_router, moe_w_gate, moe_w_up, moe_w_down,
        final_norm_g)
    return (y_prompt, y_sample, state_conv_prompt, state_conv_sample, state_gmlp_v_sample)
```

```python
import functools

import jax
import jax.numpy as jnp
from jax import lax
from jax.experimental import pallas as pl
from jax.experimental.pallas import tpu as pltpu

D_MODEL = 1024
W_A = 512
W_B = 512
N_HEADS_A = 8
HEAD_DIM_A = W_A // N_HEADS_A
GMLP_CHUNK = 128
CONV_WIDTH = 31
CONV_STATE = CONV_WIDTH - 1
HALO = 32
N_EXPERTS = 8
RMS_EPS = 1e-6
LN_EPS = 1e-5
LANES = 128
VMEM_LIMIT = 56 * 1024 * 1024

F32 = jnp.float32
BF16 = jnp.bfloat16


def _rms(x, g):
    return x * lax.rsqrt(jnp.mean(x * x, axis=-1, keepdims=True) + RMS_EPS) * g


def _layernorm(x, g, b):
    mu = jnp.mean(x, axis=-1, keepdims=True)
    xc = x - mu
    return xc * lax.rsqrt(jnp.mean(xc * xc, axis=-1, keepdims=True) + LN_EPS) * g + b


def _gelu_tanh(x):
    return 0.5 * x * (1.0 + jnp.tanh(0.7978845608028654 * (x + 0.044715 * (x * x * x))))


def _silu(x):
    return x * jax.nn.sigmoid(x)


def _dot(a, b):
    return jnp.dot(a, b, preferred_element_type=F32)


def _mixer_body(x_ref, prev_ref, ng_ref, win_ref, lg_ref, lb_ref, ws_ref, bias_ref,
                cw_ref, cb_ref, clg_ref, clb_ref, wout_ref, *rest, ts, chunk, emit_v):
    if emit_v:
        xo_ref, st_ref, v_ref, z_scr, hbuf, m_scr = rest
    else:
        xo_ref, st_ref, z_scr, hbuf, m_scr = rest
        v_ref = None

    @pl.when(pl.program_id(1) == 0)
    def _():
        hbuf[0:HALO, :] = prev_ref[...]

    x = x_ref[...]
    h = _rms(x, ng_ref[...]).astype(BF16)
    z_scr[...] = _dot(h, win_ref[...])

    row = lax.broadcasted_iota(jnp.int32, (chunk, chunk), 0)
    col = lax.broadcasted_iota(jnp.int32, (chunk, chunk), 1)
    causal = row >= col
    w_sp = [jnp.where(causal, ws_ref[hd], 0.0).astype(BF16) for hd in range(N_HEADS_A)]
    first_head = lax.broadcasted_iota(jnp.int32, (chunk, LANES), 1) < HEAD_DIM_A

    for c in range(ts // chunk):
        r0 = c * chunk
        ga = _gelu_tanh(z_scr[r0:r0 + chunk, 0:2 * W_A])
        u = ga[:, :W_A]
        v = _layernorm(ga[:, W_A:], lg_ref[...], lb_ref[...])
        if emit_v:
            v_ref[r0:r0 + chunk, :] = v
        vb = v.astype(BF16)
        parts = []
        for p in range(N_HEADS_A // 2):
            rhs = vb[:, p * LANES:(p + 1) * LANES]
            parts.append(jnp.where(first_head, _dot(w_sp[2 * p], rhs), _dot(w_sp[2 * p + 1], rhs)))
        sp = jnp.concatenate(parts, axis=1) + bias_ref[...]
        m_scr[r0:r0 + chunk, 0:W_A] = (u * sp).astype(BF16)

        zb = z_scr[r0:r0 + chunk, 2 * W_A:2 * W_A + 2 * W_B]
        hbuf[HALO + r0:HALO + r0 + chunk, :] = zb[:, :W_B] * jax.nn.sigmoid(zb[:, W_B:])
        base = HALO - CONV_STATE + r0
        acc = cw_ref[0:1, :] * hbuf[base:base + chunk, :]
        for k in range(1, CONV_WIDTH):
            acc = acc + cw_ref[k:k + 1, :] * hbuf[base + k:base + k + chunk, :]
        y = _layernorm(acc + cb_ref[...], clg_ref[...], clb_ref[...])
        m_scr[r0:r0 + chunk, W_A:W_A + W_B] = _silu(y).astype(BF16)

    xo_ref[...] = x + _dot(m_scr[...], wout_ref[...])
    tail = hbuf[ts:ts + HALO, :]
    st_ref[...] = tail
    hbuf[0:HALO, :] = tail


def _mixer(xbuf, row0, n_streams, seq, ts, chunk, prev, p, emit_v, alias):
    n_rows = xbuf.shape[0]
    steps = seq // ts
    blk0 = row0 // ts
    x_map = lambda b, s: (blk0 + b * steps + s, 0)
    const = lambda b, s: (0, 0)
    in_specs = [
        pl.BlockSpec((ts, D_MODEL), x_map),
        pl.BlockSpec((None, HALO, W_B), lambda b, s: (b, 0, 0)),
        pl.BlockSpec((1, D_MODEL), const),
        pl.BlockSpec((D_MODEL, 2 * W_A + 2 * W_B), const),
        pl.BlockSpec((1, W_A), const),
        pl.BlockSpec((1, W_A), const),
        pl.BlockSpec((N_HEADS_A, chunk, chunk), lambda b, s: (0, 0, 0)),
        pl.BlockSpec((chunk, W_A), const),
        pl.BlockSpec((CONV_WIDTH, W_B), const),
        pl.BlockSpec((1, W_B), const),
        pl.BlockSpec((1, W_B), const),
        pl.BlockSpec((1, W_B), const),
        pl.BlockSpec((W_A + W_B, D_MODEL), const),
    ]
    out_shape = [jax.ShapeDtypeStruct((n_rows, D_MODEL), F32),
                 jax.ShapeDtypeStruct((n_streams, HALO, W_B), F32)]
    out_specs = [pl.BlockSpec((ts, D_MODEL), x_map),
                 pl.BlockSpec((None, HALO, W_B), lambda b, s: (b, 0, 0))]
    if emit_v:
        out_shape.append(jax.ShapeDtypeStruct((n_streams * seq, W_A), F32))
        out_specs.append(pl.BlockSpec((ts, W_A), lambda b, s: (b * steps + s, 0)))
    args = [xbuf, prev, p['norm_mix_g'], p['w_in'], p['gmlp_ln_g'], p['gmlp_ln_b'], p['ws'], p['bias'],
            p['conv_w'], p['conv_b'], p['conv_ln_g'], p['conv_ln_b'], p['w_out']]
    extra = {}
    if alias is not None:
        args.append(alias)
        in_specs.append(pl.BlockSpec(memory_space=pl.ANY))
        extra['input_output_aliases'] = {len(args) - 1: 0}
        body = lambda *refs, **kw: _mixer_body(*refs[:13], *refs[14:], **kw)
    else:
        body = _mixer_body
    return pl.pallas_call(
        functools.partial(body, ts=ts, chunk=chunk, emit_v=emit_v),
        grid=(n_streams, steps),
        in_specs=in_specs,
        out_specs=out_specs,
        out_shape=out_shape,
        scratch_shapes=[pltpu.VMEM((ts, 2 * W_A + 2 * W_B), F32),
                        pltpu.VMEM((HALO + ts, W_B), F32),
                        pltpu.VMEM((ts, W_A + W_B), BF16)],
        compiler_params=pltpu.CompilerParams(
            dimension_semantics=("arbitrary", "arbitrary"), vmem_limit_bytes=VMEM_LIMIT),
        name="mixer_sample" if emit_v else "mixer_prompt",
        **extra,
    )(*args)


def _ffn_body(x_ref, g_ref, wg_ref, wu_ref, wd_ref, o_ref, *, ff_chunk):
    x = x_ref[...]
    h = _rms(x, g_ref[...]).astype(BF16)
    acc = x
    d_ff = wg_ref.shape[1]
    for f0 in range(0, d_ff, ff_chunk):
        a = _silu(_dot(h, wg_ref[:, f0:f0 + ff_chunk])) * _dot(h, wu_ref[:, f0:f0 + ff_chunk])
        acc = acc + _dot(a.astype(BF16), wd_ref[f0:f0 + ff_chunk, :])
    o_ref[...] = acc


def _ffn_dense(xbuf, g, wg, wu, wd, tm):
    n_rows = xbuf.shape[0]
    d_ff = wg.shape[1]
    const = lambda i: (0, 0)
    return pl.pallas_call(
        functools.partial(_ffn_body, ff_chunk=d_ff // 2),
        grid=(n_rows // tm,),
        in_specs=[pl.BlockSpec((tm, D_MODEL), lambda i: (i, 0)),
                  pl.BlockSpec((1, D_MODEL), const),
                  pl.BlockSpec((D_MODEL, d_ff), const),
                  pl.BlockSpec((D_MODEL, d_ff), const),
                  pl.BlockSpec((d_ff, D_MODEL), const)],
        out_specs=pl.BlockSpec((tm, D_MODEL), lambda i: (i, 0)),
        out_shape=jax.ShapeDtypeStruct((n_rows, D_MODEL), F32),
        compiler_params=pltpu.CompilerParams(
            dimension_semantics=("arbitrary",), vmem_limit_bytes=VMEM_LIMIT),
        name="ffn_dense",
    )(xbuf, g, wg, wu, wd)


def _route(h, router):
    logits = jnp.dot(h, router, preferred_element_type=F32, precision=lax.Precision.HIGHEST)
    idx = lax.broadcasted_iota(jnp.int32, logits.shape, 1)
    m1 = jnp.max(logits, axis=-1, keepdims=True)
    i1 = jnp.min(jnp.where(logits == m1, idx, N_EXPERTS), axis=-1, keepdims=True)
    rest = jnp.where(idx == i1, -jnp.inf, logits)
    m2 = jnp.max(rest, axis=-1, keepdims=True)
    i2 = jnp.min(jnp.where(rest == m2, idx, N_EXPERTS), axis=-1, keepdims=True)
    e2 = jnp.exp(m2 - m1)
    den = 1.0 + e2
    return jnp.where(idx == i1, 1.0 / den, 0.0) + jnp.where(idx == i2, e2 / den, 0.0)


def _moe_dense_body(x_ref, g_ref, r_ref, wg_ref, wu_ref, wd_ref, fg_ref, o_ref, h_scr, gate_scr, acc_scr, *, final):
    e = pl.program_id(1)
    f = pl.program_id(2)

    @pl.when(jnp.logical_and(e == 0, f == 0))
    def _():
        x = x_ref[...]
        h = _rms(x, g_ref[...])
        gate_scr[...] = _route(h, r_ref[...])
        h_scr[...] = h.astype(BF16)
        acc_scr[...] = x

    hb = h_scr[...]
    a = _silu(_dot(hb, wg_ref[...])) * _dot(hb, wu_ref[...])
    y = _dot(a.astype(BF16), wd_ref[...])
    gates = gate_scr[...]
    sel = lax.broadcasted_iota(jnp.int32, gates.shape, 1) == e
    acc_scr[...] += jnp.sum(jnp.where(sel, gates, 0.0), axis=-1, keepdims=True) * y

    @pl.when(jnp.logical_and(e == pl.num_programs(1) - 1, f == pl.num_programs(2) - 1))
    def _():
        o_ref[...] = _rms(acc_scr[...], fg_ref[...]) if final else acc_scr[...]


def _moe_dense(xbuf, g, router, wg, wu, wd, final_g, final, tm, tf):
    n_rows = xbuf.shape[0]
    d_ff = wg.shape[2]
    const = lambda i, e, f: (0, 0)
    return pl.pallas_call(
        functools.partial(_moe_dense_body, final=final),
        grid=(n_rows // tm, N_EXPERTS, d_ff // tf),
        in_specs=[pl.BlockSpec((tm, D_MODEL), lambda i, e, f: (i, 0)),
                  pl.BlockSpec((1, D_MODEL), const),
                  pl.BlockSpec((D_MODEL, N_EXPERTS), const),
                  pl.BlockSpec((None, D_MODEL, tf), lambda i, e, f: (e, 0, f)),
                  pl.BlockSpec((None, D_MODEL, tf), lambda i, e, f: (e, 0, f)),
                  pl.BlockSpec((None, tf, D_MODEL), lambda i, e, f: (e, f, 0)),
                  pl.BlockSpec((1, D_MODEL), const)],
        out_specs=pl.BlockSpec((tm, D_MODEL), lambda i, e, f: (i, 0)),
        out_shape=jax.ShapeDtypeStruct((n_rows, D_MODEL), F32),
        scratch_shapes=[pltpu.VMEM((tm, D_MODEL), BF16),
                        pltpu.VMEM((tm, N_EXPERTS), F32),
                        pltpu.VMEM((tm, D_MODEL), F32)],
        compiler_params=pltpu.CompilerParams(
            dimension_semantics=("arbitrary", "arbitrary", "arbitrary"), vmem_limit_bytes=VMEM_LIMIT),
        name="moe_dense",
    )(xbuf, g, router, wg, wu, wd, final_g)


def _layer_params(l, chunk, norm_mix_g, w_in, gmlp_ln_g, gmlp_ln_b, gmlp_ws, gmlp_bs,
                  conv_w, conv_b, conv_ln_g, conv_ln_b, w_out):
    bias = jnp.repeat(jnp.transpose(gmlp_bs[l][:, :chunk]), HEAD_DIM_A, axis=1)
    return dict(
        norm_mix_g=norm_mix_g[l][None], w_in=w_in[l].astype(BF16),
        gmlp_ln_g=gmlp_ln_g[l][None], gmlp_ln_b=gmlp_ln_b[l][None],
        ws=gmlp_ws[l][:, :chunk, :chunk], bias=bias,
        conv_w=conv_w[l], conv_b=conv_b[l][None],
        conv_ln_g=conv_ln_g[l][None], conv_ln_b=conv_ln_b[l][None],
        w_out=w_out[l].astype(BF16))


def _forward(x_prompt, x_sample, cache_conv, norm_mix_g, w_in, gmlp_ln_g, gmlp_ln_b, gmlp_ws,
             gmlp_bs, conv_w, conv_b, conv_ln_g, conv_ln_b, w_out, norm_ffn_g,
             ffn_w_gate, ffn_w_up, ffn_w_down, moe_router, moe_w_gate, moe_w_up, moe_w_down,
             final_norm_g, *, ts_prompt, tm):
    depth = w_in.shape[0]
    assert depth % 2 == 0, "the final RMSNorm is fused into the last (mixture) layer"
    bp, sp, _ = x_prompt.shape
    bs, ss, _ = x_sample.shape
    n_prompt = bp * sp
    n_sample = bs * ss
    chunk_p = min(sp, GMLP_CHUNK)
    chunk_s = min(ss, GMLP_CHUNK)
    mix_w = (norm_mix_g, w_in, gmlp_ln_g, gmlp_ln_b, gmlp_ws, gmlp_bs,
             conv_w, conv_b, conv_ln_g, conv_ln_b, w_out)

    xbuf = jnp.concatenate([x_prompt.reshape(n_prompt, D_MODEL), x_sample.reshape(n_sample, D_MODEL)], axis=0)
    prev_p = jnp.zeros((bp, HALO, W_B), F32)
    pad = HALO - CONV_STATE
    st_p, st_s, v_s = [], [], []
    for l in range(depth):
        prev_s = jnp.pad(cache_conv[l], ((0, 0), (pad, 0), (0, 0)))
        xmid, st = _mixer(xbuf, 0, bp, sp, ts_prompt, chunk_p, prev_p,
                          _layer_params(l, chunk_p, *mix_w), False, None)
        st_p.append(st[:, pad:, :])
        xmid, st, v = _mixer(xbuf, n_prompt, bs, ss, ss, chunk_s, prev_s,
                             _layer_params(l, chunk_s, *mix_w), True, xmid)
        st_s.append(st[:, pad:, :])
        v_s.append(v.reshape(bs, ss, W_A))
        i = l // 2
        g = norm_ffn_g[l][None]
        if l % 2 == 0:
            xbuf = _ffn_dense(xmid, g, ffn_w_gate[i].astype(BF16), ffn_w_up[i].astype(BF16),
                              ffn_w_down[i].astype(BF16), tm)
        else:
            xbuf = _moe_dense(xmid, g, moe_router[i], moe_w_gate[i].astype(BF16),
                              moe_w_up[i].astype(BF16), moe_w_down[i].astype(BF16),
                              final_norm_g[None], l == depth - 1, tm, 512)
    y_prompt = xbuf[:n_prompt].reshape(bp, sp, D_MODEL)
    y_sample = xbuf[n_prompt:].reshape(bs, ss, D_MODEL)
    return (y_prompt, y_sample, jnp.stack(st_p), jnp.stack(st_s), jnp.stack(v_s))


def kernel(x_prompt, x_sample, cache_conv, norm_mix_g, w_in, gmlp_ln_g, gmlp_ln_b, gmlp_ws, gmlp_bs, conv_w, conv_b, conv_ln_g, conv_ln_b, w_out, norm_ffn_g, ffn_w_gate, ffn_w_up, ffn_w_down, moe_router, moe_w_gate, moe_w_up, moe_w_down, final_norm_g):
    return _forward(x_prompt, x_sample, cache_conv, norm_mix_g, w_in, gmlp_ln_g, gmlp_ln_b, gmlp_ws,
                    gmlp_bs, conv_w, conv_b, conv_ln_g, conv_ln_b, w_out, norm_ffn_g,
                    ffn_w_gate, ffn_w_up, ffn_w_down, moe_router, moe_w_gate, moe_w_up, moe_w_down,
                    final_norm_g, ts_prompt=256, tm=512)
```
